```python
import math
import jax, jax.numpy as jnp
from jax import lax
import numpy as np

D_MODEL = 1024
BATCH = 8
SEQ = 2048
DEPTH = 2
DEC_BATCH = 128
DEC_SEQ = 1
PAST_LEN = 16384
PAGE_SIZE = 128

D_MIX = D_MODEL
N_MIXERS = 4
GROUP_W = D_MIX // N_MIXERS
HEADS_PER_MIXER = 4
HEAD_DIM = GROUP_W // HEADS_PER_MIXER
IN_WIDTH = 8 * GROUP_W
CONV_A_WIDTH = 31
CONV_D_WIDTH = 3
POOL_WINDOWS = (2, 4, 8, 16)
POOL_BUF = max(POOL_WINDOWS) - 1
CHUNK = 128
D_FF = -(-8 * D_MODEL // (3 * 256)) * 256
N_MOD = 6
RMS_EPS = 1e-6
LN_EPS = 1e-5

kernel_name = 'hymba_conv_pool_gmlp_shortconv_decoder'


def rmsnorm(x, g):
    xf = x.astype(jnp.float32)
    y = xf * lax.rsqrt(jnp.mean(xf * xf, axis=-1, keepdims=True) + RMS_EPS)
    return (y * g.astype(jnp.float32)).astype(x.dtype)


def layernorm(x, g, b):
    xf = x.astype(jnp.float32)
    mu = jnp.mean(xf, axis=-1, keepdims=True)
    var = jnp.mean(jnp.square(xf - mu), axis=-1, keepdims=True)
    y = (xf - mu) * lax.rsqrt(var + LN_EPS)
    return (y * g.astype(jnp.float32) + b.astype(jnp.float32)).astype(x.dtype)


def causal_depthwise(z_ext, w):
    c = z_ext.shape[-1]
    return lax.conv_general_dilated(z_ext, w[:, None, :].astype(z_ext.dtype), (1,), 'VALID',
                                    dimension_numbers=('NWC', 'WIO', 'NWC'),
                                    feature_group_count=c)


def multiscale_pool(z_ext, n_prev, w_lin, scale):
    bsz, length, c = z_ext.shape
    t_new = length - n_prev
    gc = c // len(POOL_WINDOWS)
    zf = z_ext.astype(jnp.float32)
    csum = jnp.concatenate([jnp.zeros((bsz, 1, c), jnp.float32), jnp.cumsum(zf, axis=1)], axis=1)
    j = np.arange(n_prev, length)
    outs = []
    for g, w in enumerate(POOL_WINDOWS):
        s_g = csum[..., g * gc:(g + 1) * gc]
        start = np.maximum(j + 1 - w, 0)
        cnt = jnp.asarray((j + 1 - start).astype(np.float32))
        mean = (jnp.take(s_g, j + 1, axis=1) - jnp.take(s_g, start, axis=1)) / cnt[None, :, None]
        outs.append(mean - zf[:, n_prev:, g * gc:(g + 1) * gc])
    d = jnp.stack(outs, axis=2).astype(z_ext.dtype)
    y = jnp.einsum('btgc,gcd->btgd', d, w_lin).reshape(bsz, t_new, c)
    return y * scale


def spatial_gating(v, w_s, b_s):
    bsz, t, c = v.shape
    nc = -(-t // CHUNK)
    vp = jnp.pad(v, ((0, 0), (0, nc * CHUNK - t), (0, 0)))
    vp = vp.reshape(bsz, nc, CHUNK, HEADS_PER_MIXER, c // HEADS_PER_MIXER)
    mask = np.tril(np.ones((CHUNK, CHUNK), dtype=bool))
    wm = jnp.where(mask[None], w_s, jnp.zeros_like(w_s))
    out = jnp.einsum('hij,bnjhc->bnihc', wm, vp) + b_s.T[None, None, :, :, None]
    return out.reshape(bsz, nc * CHUNK, c)[:, :t]


def token_mixer(h, buf_a, buf_b, buf_d, p):
    proj = h @ p['w_in']
    a_val, a_gate, b_in, c_u, c_v, d_b, d_c, d_h = jnp.split(proj, 8, axis=-1)
    za = a_val * jax.nn.sigmoid(a_gate)
    za_ext = jnp.concatenate([buf_a, za], axis=1)
    ya = causal_depthwise(za_ext, p['conv_a_w']) + p['conv_a_b']
    ya = jax.nn.silu(layernorm(ya, p['ln_a_g'], p['ln_a_b']))
    zb_ext = jnp.concatenate([buf_b, b_in], axis=1)
    yb = multiscale_pool(zb_ext, buf_b.shape[1], p['pool_w'], p['pool_scale'])
    vn = layernorm(c_v, p['ln_c_g'], p['ln_c_b'])
    yc = c_u * spatial_gating(vn, p['sgu_w'], p['sgu_b'])
    zd_ext = jnp.concatenate([buf_d, d_c * d_h], axis=1)
    yd = d_b * causal_depthwise(zd_ext, p['conv_d_w'])
    out = jnp.concatenate([ya, yb, yc, yd], axis=-1) @ p['w_out']
    return (out, za_ext[:, -(CONV_A_WIDTH - 1):], zb_ext[:, -POOL_BUF:],
            zd_ext[:, -(CONV_D_WIDTH - 1):], vn)


def decoder_layer(x, c, buf_a, buf_b, buf_d, p):
    mod = (jax.nn.silu(c) @ p['w_ada'] + p['b_ada'])[:, None, :]
    sh1, sc1, gt1, sh2, sc2, gt2 = jnp.split(mod, N_MOD, axis=-1)
    h = rmsnorm(x, p['g_pre_mix']) * (1 + sc1) + sh1
    m, na, nb, nd, vn = token_mixer(h, buf_a, buf_b, buf_d, p)
    x = x + gt1 * rmsnorm(m, p['g_post_mix'])
    h = rmsnorm(x, p['g_pre_ffn']) * (1 + sc2) + sh2
    f = (jax.nn.silu(h @ p['w_gate']) * (h @ p['w_up'])) @ p['w_down']
    x = x + gt2 * rmsnorm(f, p['g_post_ffn'])
    return x, na, nb, nd, vn


def setup_inputs(seed: int = 0) -> dict:
    key = jax.random.key(seed)
    ks = jax.random.split(key, 32)
    nrm = jax.random.normal
    f32 = jnp.float32
    L, D, G = DEPTH, D_MODEL, GROUP_W
    def gain(k, shape):
        return 1.0 + 0.05 * nrm(k, shape, f32)
    return {
        'x_prompt': nrm(ks[0], (BATCH, SEQ, D), f32),
        'x_sample': nrm(ks[1], (DEC_BATCH, DEC_SEQ, D), f32),
        'c_prompt': nrm(ks[2], (BATCH, D), f32),
        'c_sample': nrm(ks[3], (DEC_BATCH, D), f32),
        'state_conv_a': 0.5 * nrm(ks[4], (L, DEC_BATCH, CONV_A_WIDTH - 1, G), f32),
        'state_pool_b': nrm(ks[5], (L, DEC_BATCH, POOL_BUF, G), f32),
        'state_conv_d': 0.5 * nrm(ks[6], (L, DEC_BATCH, CONV_D_WIDTH - 1, G), f32),
        'w_ada': nrm(ks[7], (L, D, N_MOD * D), f32) * D ** -0.5,
        'b_ada': 0.01 * nrm(ks[8], (L, N_MOD * D), f32),
        'g_pre_mix': gain(ks[9], (L, D)),
        'g_post_mix': gain(ks[10], (L, D)),
        'w_in': nrm(ks[11], (L, D, IN_WIDTH), f32) * D ** -0.5,
        'conv_a_w': nrm(ks[12], (L, CONV_A_WIDTH, G), f32) * CONV_A_WIDTH ** -0.5,
        'conv_a_b': 0.02 * nrm(ks[13], (L, G), f32),
        'ln_a_g': gain(ks[14], (L, G)),
        'ln_a_b': 0.02 * nrm(ks[15], (L, G), f32),
        'pool_w': nrm(ks[16], (L, len(POOL_WINDOWS), G // len(POOL_WINDOWS), G // len(POOL_WINDOWS)), f32) * (G // len(POOL_WINDOWS)) ** -0.5,
        'pool_scale': gain(ks[17], (L, G)),
        'ln_c_g': gain(ks[18], (L, G)),
        'ln_c_b': 0.02 * nrm(ks[19], (L, G), f32),
        'sgu_w': nrm(ks[20], (L, HEADS_PER_MIXER, CHUNK, CHUNK), f32) * CHUNK ** -0.5,
        'sgu_b': 1.0 + 0.1 * nrm(ks[21], (L, HEADS_PER_MIXER, CHUNK), f32),
        'conv_d_w': nrm(ks[22], (L, CONV_D_WIDTH, G), f32) * CONV_D_WIDTH ** -0.5,
        'w_out': nrm(ks[23], (L, D_MIX, D), f32) * D_MIX ** -0.5,
        'g_pre_ffn': gain(ks[24], (L, D)),
        'g_post_ffn': gain(ks[25], (L, D)),
        'w_gate': nrm(ks[26], (L, D, D_FF), f32) * D ** -0.5,
        'w_up': nrm(ks[27], (L, D, D_FF), f32) * D ** -0.5,
        'w_down': nrm(ks[28], (L, D_FF, D), f32) * D_FF ** -0.5,
    }


def reference(x_prompt, x_sample, c_prompt, c_sample, state_conv_a, state_pool_b, state_conv_d,
              w_ada, b_ada, g_pre_mix, g_post_mix, w_in, conv_a_w, conv_a_b, ln_a_g, ln_a_b,
              pool_w, pool_scale, ln_c_g, ln_c_b, sgu_w, sgu_b, conv_d_w, w_out,
              g_pre_ffn, g_post_ffn, w_gate, w_up, w_down):
    bp, dt = x_prompt.shape[0], x_prompt.dtype
    buf_a_p = jnp.zeros((bp, CONV_A_WIDTH - 1, GROUP_W), dt)
    buf_b_p = jnp.zeros((bp, 0, GROUP_W), dt)
    buf_d_p = jnp.zeros((bp, CONV_D_WIDTH - 1, GROUP_W), dt)
    xp, xs = x_prompt, x_sample
    na_p, nb_p, nd_p, na_s, nb_s, nd_s, nv_s = [], [], [], [], [], [], []
    for l in range(DEPTH):
        p = {'w_ada': w_ada[l], 'b_ada': b_ada[l], 'g_pre_mix': g_pre_mix[l], 'g_post_mix': g_post_mix[l],
             'w_in': w_in[l], 'conv_a_w': conv_a_w[l], 'conv_a_b': conv_a_b[l], 'ln_a_g': ln_a_g[l],
             'ln_a_b': ln_a_b[l], 'pool_w': pool_w[l], 'pool_scale': pool_scale[l], 'ln_c_g': ln_c_g[l],
             'ln_c_b': ln_c_b[l], 'sgu_w': sgu_w[l], 'sgu_b': sgu_b[l], 'conv_d_w': conv_d_w[l],
             'w_out': w_out[l], 'g_pre_ffn': g_pre_ffn[l], 'g_post_ffn': g_post_ffn[l],
             'w_gate': w_gate[l], 'w_up': w_up[l], 'w_down': w_down[l]}
        xp, a_p, b_p, d_p, _ = decoder_layer(xp, c_prompt, buf_a_p, buf_b_p, buf_d_p, p)
        xs, a_s, b_s, d_s, v_s = decoder_layer(xs, c_sample, state_conv_a[l], state_pool_b[l],
                                               state_conv_d[l], p)
        na_p.append(a_p); nb_p.append(b_p); nd_p.append(d_p)
        na_s.append(a_s); nb_s.append(b_s); nd_s.append(d_s); nv_s.append(v_s)
    return (xp, xs,
            jnp.stack(na_p), jnp.stack(nb_p), jnp.stack(nd_p),
            jnp.stack(na_s), jnp.stack(nb_s), jnp.stack(nd_s), jnp.stack(nv_s))
```

```python
import functools

import jax
import jax.numpy as jnp
from jax import lax
from jax.experimental import pallas as pl
from jax.experimental.pallas import tpu as pltpu

D_MODEL = 1024
GROUP_W = 256
HEADS = 4
HEAD_DIM = GROUP_W // HEADS
HEAD_SHIFT = HEAD_DIM.bit_length() - 1
IN_WIDTH = 8 * GROUP_W
CONV_A = 31
CONV_D = 3
POOL_WINDOWS = (2, 4, 8, 16)
POOL_BUF = 15
CHUNK = 128
D_FF = 2816
N_MOD = 6
RMS_EPS = 1e-6
LN_EPS = 1e-5

SUBLANES = 8
LANES = 128

TOKEN_TILE = 256
CONV_ROWS = 64
HIST_A = 32
HIST_B = 16
HIST_D = 8
ADA_TILE_N = 1024
VMEM_LIMIT_BYTES = 56 * 1024 * 1024

_BF16 = jnp.bfloat16
_F32 = jnp.float32


def _dot(a, w):
    return jnp.dot(a.astype(_BF16), w, preferred_element_type=_F32)


def _rms(x, g):
    return x * lax.rsqrt(jnp.mean(x * x, axis=-1, keepdims=True) + RMS_EPS) * g


def _ln(x, g, b):
    mu = jnp.mean(x, axis=-1, keepdims=True)
    xc = x - mu
    var = jnp.mean(xc * xc, axis=-1, keepdims=True)
    return xc * lax.rsqrt(var + LN_EPS) * g + b


def _silu(x):
    return x * jax.nn.sigmoid(x)


def _split_proj(proj):
    return [proj[:, i * GROUP_W:(i + 1) * GROUP_W] for i in range(8)]


def _out_and_ffn(x, mods, ys, g_post_mix, g_pre_ffn, g_post_ffn, wout_ref, wg_ref, wu_ref, wd_ref):
    _, _, gt1, sh2, sc2, gt2 = mods
    y = jnp.concatenate([v.astype(_BF16) for v in ys], axis=-1)
    m = jnp.dot(y, wout_ref[...], preferred_element_type=_F32)
    x = x + gt1 * _rms(m, g_post_mix)
    h = (_rms(x, g_pre_ffn) * (1.0 + sc2) + sh2).astype(_BF16)
    gate = jnp.dot(h, wg_ref[...], preferred_element_type=_F32)
    up = jnp.dot(h, wu_ref[...], preferred_element_type=_F32)
    f = _dot(_silu(gate) * up, wd_ref[...])
    return x + gt2 * _rms(f, g_post_ffn)


def _ada_kernel(c_ref, w_ref, b_ref, o_ref):
    s = _silu(c_ref[...])
    o_ref[...] = _dot(s, w_ref[...].astype(_BF16)) + b_ref[...]


def _ada_call(c_all, w_ada, b_ada):
    n_layers = w_ada.shape[0]
    rows = c_all.shape[0]
    n_out = N_MOD * D_MODEL
    return pl.pallas_call(
        _ada_kernel,
        grid=(n_layers, n_out // ADA_TILE_N),
        in_specs=[
            pl.BlockSpec((rows, D_MODEL), lambda l, j: (0, 0)),
            pl.BlockSpec((None, D_MODEL, ADA_TILE_N), lambda l, j: (l, 0, j)),
            pl.BlockSpec((None, 1, ADA_TILE_N), lambda l, j: (l, 0, j)),
        ],
        out_specs=pl.BlockSpec((None, rows, ADA_TILE_N), lambda l, j: (l, 0, j)),
        out_shape=jax.ShapeDtypeStruct((n_layers, rows, n_out), _F32),
        compiler_params=pltpu.CompilerParams(dimension_semantics=("arbitrary", "arbitrary")),
        name="ada_mod",
    )(c_all, w_ada, b_ada.reshape(n_layers, 1, n_out))


def _prompt_kernel(x_ref, mod_ref, gpm_ref, gqm_ref, gpf_ref, gqf_ref, win_ref, caw_ref, cab_ref,
                   lag_ref, lab_ref, pw_ref, ps_ref, lcg_ref, lcb_ref, sw_ref, sbias_ref, cdw_ref,
                   wout_ref, wg_ref, wu_ref, wd_ref,
                   xo_ref, na_ref, nb_ref, nd_ref, sa, sb, sd):
    tm = TOKEN_TILE
    b = pl.program_id(0)
    t = pl.program_id(1)

    @pl.when(t == 0)
    def _():
        sa[0:HIST_A, :] = jnp.zeros((HIST_A, GROUP_W), _F32)
        sb[0:HIST_B, :] = jnp.zeros((HIST_B, GROUP_W), _F32)
        sd[0:HIST_D, :] = jnp.zeros((HIST_D, GROUP_W), _F32)

    x = x_ref[...]
    mod = mod_ref[pl.ds(b, 1), :]
    mods = [mod[:, i * D_MODEL:(i + 1) * D_MODEL] for i in range(N_MOD)]
    sh1, sc1 = mods[0], mods[1]

    h = _rms(x, gpm_ref[...]) * (1.0 + sc1) + sh1
    proj = _dot(h, win_ref[...])
    a_val, a_gate, b_in, c_u, c_v, d_b, d_c, d_h = _split_proj(proj)

    sa[HIST_A:HIST_A + tm, :] = a_val * jax.nn.sigmoid(a_gate)
    conv_blocks = []
    for r in range(0, tm, CONV_ROWS):
        acc = jnp.broadcast_to(cab_ref[...], (CONV_ROWS, GROUP_W))
        for k in range(CONV_A):
            acc = acc + caw_ref[k:k + 1, :] * sa[pl.ds(HIST_A - (CONV_A - 1) + k + r, CONV_ROWS), :]
        conv_blocks.append(acc)
    ya = _silu(_ln(jnp.concatenate(conv_blocks, axis=0), lag_ref[...], lab_ref[...]))

    sb[HIST_B:HIST_B + tm, :] = b_in
    pos1 = (t * tm + 1 + lax.broadcasted_iota(jnp.int32, (tm, LANES), 0))
    lane = lax.broadcasted_iota(jnp.int32, (tm, LANES), 1)
    low_half = lane < HEAD_DIM
    diffs = []
    for col, (w_lo, w_hi) in enumerate(((POOL_WINDOWS[0], POOL_WINDOWS[1]), (POOL_WINDOWS[2], POOL_WINDOWS[3]))):
        lanes = slice(col * LANES, (col + 1) * LANES)
        s_lo = sb[pl.ds(HIST_B, tm), lanes]
        for j in range(1, w_lo):
            s_lo = s_lo + sb[pl.ds(HIST_B - j, tm), lanes]
        s_hi = s_lo
        for j in range(w_lo, w_hi):
            s_hi = s_hi + sb[pl.ds(HIST_B - j, tm), lanes]
        cnt = jnp.where(low_half, jnp.minimum(pos1, w_lo), jnp.minimum(pos1, w_hi)).astype(_F32)
        diffs.append(jnp.where(low_half, s_lo, s_hi) / cnt - b_in[:, lanes])
    yb = _dot(jnp.concatenate(diffs, axis=-1), pw_ref[...]) * ps_ref[...]

    vn = _ln(c_v, lcg_ref[...], lcb_ref[...])
    row_i = lax.broadcasted_iota(jnp.int32, (CHUNK, HEADS * CHUNK), 0)
    col_j = lax.broadcasted_iota(jnp.int32, (CHUNK, HEADS * CHUNK), 1) & (CHUNK - 1)
    w_tril = jnp.where(col_j <= row_i, sw_ref[...], 0.0).astype(_BF16)
    head_of_lane = lax.broadcasted_iota(jnp.int32, (CHUNK, GROUP_W), 1) >> HEAD_SHIFT
    gated = []
    for ci in range(tm // CHUNK):
        v = vn[ci * CHUNK:(ci + 1) * CHUNK, :]
        stacked = jnp.concatenate(
            [jnp.where(head_of_lane == hd, v, 0.0).astype(_BF16) for hd in range(HEADS)], axis=0)
        gated.append(jnp.dot(w_tril, stacked, preferred_element_type=_F32) + sbias_ref[...])
    yc = c_u * jnp.concatenate(gated, axis=0)

    q = d_c * d_h
    sd[HIST_D:HIST_D + tm, :] = q
    yd = d_b * (cdw_ref[0:1, :] * sd[pl.ds(HIST_D - 2, tm), :]
                + cdw_ref[1:2, :] * sd[pl.ds(HIST_D - 1, tm), :]
                + cdw_ref[2:3, :] * q)

    @pl.when(t == pl.num_programs(1) - 1)
    def _():
        na_ref[...] = sa[pl.ds(HIST_A + tm - (CONV_A - 1), CONV_A - 1), :]
        nb_ref[...] = sb[pl.ds(HIST_B + tm - POOL_BUF, POOL_BUF), :]
        nd_ref[...] = sd[pl.ds(HIST_D + tm - (CONV_D - 1), CONV_D - 1), :]

    sa[0:HIST_A, :] = sa[tm:tm + HIST_A, :]
    sb[0:HIST_B, :] = sb[tm:tm + HIST_B, :]
    sd[0:HIST_D, :] = sd[tm:tm + HIST_D, :]

    xo_ref[...] = _out_and_ffn(x, mods, (ya, yb, yc, yd), gqm_ref[...], gpf_ref[...], gqf_ref[...],
                               wout_ref, wg_ref, wu_ref, wd_ref)


def _layer_spec(shape, n_grid, single_buffer=False):
    zeros = (0,) * len(shape)
    kwargs = {"pipeline_mode": pl.Buffered(1)} if single_buffer else {}

    def make(layer):
        if n_grid == 2:
            return pl.BlockSpec((None,) + tuple(shape), lambda b, t: (layer,) + zeros, **kwargs)
        return pl.BlockSpec((None,) + tuple(shape), lambda i: (layer,) + zeros, **kwargs)
    return make


def _prompt_layer(layer, x, mod, p):
    bsz, seq, _ = x.shape
    tm = TOKEN_TILE
    n_sample = mod.shape[1] - bsz

    def vec(n):
        return _layer_spec((1, n), 2)(layer)

    def mat(shape):
        return _layer_spec(shape, 2, single_buffer=True)(layer)

    in_specs = [
        pl.BlockSpec((None, tm, D_MODEL), lambda b, t: (b, t, 0)),
        pl.BlockSpec((None, bsz, N_MOD * D_MODEL), lambda b, t: (layer, n_sample // bsz, 0)),
        vec(D_MODEL), vec(D_MODEL), vec(D_MODEL), vec(D_MODEL),
        mat((D_MODEL, IN_WIDTH)),
        _layer_spec((CONV_A, GROUP_W), 2)(layer), vec(GROUP_W), vec(GROUP_W), vec(GROUP_W),
        mat((GROUP_W, GROUP_W)), vec(GROUP_W), vec(GROUP_W), vec(GROUP_W),
        _layer_spec((CHUNK, HEADS * CHUNK), 2)(layer), _layer_spec((CHUNK, GROUP_W), 2)(layer),
        _layer_spec((CONV_D, GROUP_W), 2)(layer),
        mat((D_MODEL, D_MODEL)), mat((D_MODEL, D_FF)), mat((D_MODEL, D_FF)), mat((D_FF, D_MODEL)),
    ]
    out_specs = [
        pl.BlockSpec((None, tm, D_MODEL), lambda b, t: (b, t, 0)),
        pl.BlockSpec((None, CONV_A - 1, GROUP_W), lambda b, t: (b, 0, 0)),
        pl.BlockSpec((None, POOL_BUF, GROUP_W), lambda b, t: (b, 0, 0)),
        pl.BlockSpec((None, CONV_D - 1, GROUP_W), lambda b, t: (b, 0, 0)),
    ]
    out_shape = [
        jax.ShapeDtypeStruct((bsz, seq, D_MODEL), _F32),
        jax.ShapeDtypeStruct((bsz, CONV_A - 1, GROUP_W), _F32),
        jax.ShapeDtypeStruct((bsz, POOL_BUF, GROUP_W), _F32),
        jax.ShapeDtypeStruct((bsz, CONV_D - 1, GROUP_W), _F32),
    ]
    return pl.pallas_call(
        _prompt_kernel,
        grid=(bsz, seq // tm),
        in_specs=in_specs,
        out_specs=out_specs,
        out_shape=out_shape,
        scratch_shapes=[
            pltpu.VMEM((HIST_A + tm, GROUP_W), _F32),
            pltpu.VMEM((HIST_B + tm, GROUP_W), _F32),
            pltpu.VMEM((HIST_D + tm, GROUP_W), _F32),
        ],
        compiler_params=pltpu.CompilerParams(
            dimension_semantics=("arbitrary", "arbitrary"), vmem_limit_bytes=VMEM_LIMIT_BYTES),
        name=f"prompt_layer{layer}",
    )(x, mod, p["g_pre_mix"], p["g_post_mix"], p["g_pre_ffn"], p["g_post_ffn"], p["w_in"],
      p["conv_a_w"], p["conv_a_b"], p["ln_a_g"], p["ln_a_b"], p["pool_w"], p["pool_scale"],
      p["ln_c_g"], p["ln_c_b"], p["sgu_w"], p["sgu_bias"], p["conv_d_w"],
      p["w_out"], p["w_gate"], p["w_up"], p["w_down"])


def _sample_kernel(x_ref, mod_ref, sta_ref, stb_ref, std_ref, gpm_ref, gqm_ref, gpf_ref, gqf_ref,
                   win_ref, caw_ref, cab_ref, lag_ref, lab_ref, pw_ref, ps_ref, lcg_ref, lcb_ref,
                   sw0_ref, sb0_ref, cdw_ref, wout_ref, wg_ref, wu_ref, wd_ref,
                   xo_ref, na_ref, nb_ref, nd_ref, nv_ref):
    x = x_ref[...]
    mods = [mod_ref[:, i * D_MODEL:(i + 1) * D_MODEL] for i in range(N_MOD)]
    sh1, sc1 = mods[0], mods[1]
    h = _rms(x, gpm_ref[...]) * (1.0 + sc1) + sh1
    proj = _dot(h, win_ref[...])
    a_val, a_gate, b_in, c_u, c_v, d_b, d_c, d_h = _split_proj(proj)

    za = a_val * jax.nn.sigmoid(a_gate)
    acc = cab_ref[...] + caw_ref[CONV_A - 1:CONV_A, :] * za
    for k in range(CONV_A - 1):
        acc = acc + caw_ref[k:k + 1, :] * sta_ref[k]
    ya = _silu(_ln(acc, lag_ref[...], lab_ref[...]))
    for k in range(CONV_A - 2):
        na_ref[k] = sta_ref[k + 1]
    na_ref[CONV_A - 2] = za

    lane = lax.broadcasted_iota(jnp.int32, b_in.shape, 1)
    window_sum = b_in
    pooled = jnp.zeros_like(b_in)
    j = 1
    for g, w in enumerate(POOL_WINDOWS):
        while j < w:
            window_sum = window_sum + stb_ref[POOL_BUF - j]
            j += 1
        pooled = jnp.where((lane >> HEAD_SHIFT) == g, window_sum / float(w), pooled)
    yb = _dot(pooled - b_in, pw_ref[...]) * ps_ref[...]
    for k in range(POOL_BUF - 1):
        nb_ref[k] = stb_ref[k + 1]
    nb_ref[POOL_BUF - 1] = b_in

    vn = _ln(c_v, lcg_ref[...], lcb_ref[...])
    nv_ref[...] = vn
    yc = c_u * (sw0_ref[...] * vn + sb0_ref[...])

    q = d_c * d_h
    yd = d_b * (cdw_ref[0:1, :] * std_ref[0] + cdw_ref[1:2, :] * std_ref[1] + cdw_ref[2:3, :] * q)
    nd_ref[0] = std_ref[1]
    nd_ref[1] = q

    xo_ref[...] = _out_and_ffn(x, mods, (ya, yb, yc, yd), gqm_ref[...], gpf_ref[...], gqf_ref[...],
                               wout_ref, wg_ref, wu_ref, wd_ref)


def _sample_layer(layer, x, mod, sta, stb, std, p):
    n = x.shape[0]

    def vec(width):
        return _layer_spec((1, width), 1)(layer)

    def mat(shape):
        return _layer_spec(shape, 1, single_buffer=True)(layer)

    def state(rows):
        return pl.BlockSpec((None, rows, n, GROUP_W), lambda i: (layer, 0, 0, 0))

    in_specs = [
        pl.BlockSpec((n, D_MODEL), lambda i: (0, 0)),
        pl.BlockSpec((None, n, N_MOD * D_MODEL), lambda i: (layer, 0, 0)),
        state(CONV_A - 1), state(POOL_BUF), state(CONV_D - 1),
        vec(D_MODEL), vec(D_MODEL), vec(D_MODEL), vec(D_MODEL),
        mat((D_MODEL, IN_WIDTH)),
        _layer_spec((CONV_A, GROUP_W), 1)(layer), vec(GROUP_W), vec(GROUP_W), vec(GROUP_W),
        mat((GROUP_W, GROUP_W)), vec(GROUP_W), vec(GROUP_W), vec(GROUP_W),
        vec(GROUP_W), vec(GROUP_W),
        _layer_spec((CONV_D, GROUP_W), 1)(layer),
        mat((D_MODEL, D_MODEL)), mat((D_MODEL, D_FF)), mat((D_MODEL, D_FF)), mat((D_FF, D_MODEL)),
    ]
    out_specs = [
        pl.BlockSpec((n, D_MODEL), lambda i: (0, 0)),
        pl.BlockSpec((CONV_A - 1, n, GROUP_W), lambda i: (0, 0, 0)),
        pl.BlockSpec((POOL_BUF, n, GROUP_W), lambda i: (0, 0, 0)),
        pl.BlockSpec((CONV_D - 1, n, GROUP_W), lambda i: (0, 0, 0)),
        pl.BlockSpec((n, GROUP_W), lambda i: (0, 0)),
    ]
    out_shape = [
        jax.ShapeDtypeStruct((n, D_MODEL), _F32),
        jax.ShapeDtypeStruct((CONV_A - 1, n, GROUP_W), _F32),
        jax.ShapeDtypeStruct((POOL_BUF, n, GROUP_W), _F32),
        jax.ShapeDtypeStruct((CONV_D - 1, n, GROUP_W), _F32),
        jax.ShapeDtypeStruct((n, GROUP_W), _F32),
    ]
    return pl.pallas_call(
        _sample_kernel,
        grid=(1,),
        in_specs=in_specs,
        out_specs=out_specs,
        out_shape=out_shape,
        compiler_params=pltpu.CompilerParams(
            dimension_semantics=("arbitrary",), vmem_limit_bytes=VMEM_LIMIT_BYTES),
        name=f"sample_layer{layer}",
    )(x, mod, sta, stb, std, p["g_pre_mix"], p["g_post_mix"], p["g_pre_ffn"], p["g_post_ffn"],
      p["w_in"], p["conv_a_w"], p["conv_a_b"], p["ln_a_g"], p["ln_a_b"], p["pool_w"],
      p["pool_scale"], p["ln_c_g"], p["ln_c_b"], p["sgu_w00"], p["sgu_b0"], p["conv_d_w"],
      p["w_out"], p["w_gate"], p["w_up"], p["w_down"])


def _prepare_params(g_pre_mix, g_post_mix, w_in, conv_a_w, conv_a_b, ln_a_g, ln_a_b, pool_w, pool_scale,
                    ln_c_g, ln_c_b, sgu_w, sgu_b, conv_d_w, w_out, g_pre_ffn, g_post_ffn,
                    w_gate, w_up, w_down):
    n_layers = w_in.shape[0]

    def row(v):
        return v.reshape(n_layers, 1, v.shape[-1])

    n_groups = len(POOL_WINDOWS)
    eye = jnp.eye(n_groups, dtype=pool_w.dtype)
    pool_bd = jnp.einsum("lgcd,gh->lgchd", pool_w, eye).reshape(n_layers, GROUP_W, GROUP_W)
    sgu_cat = jnp.transpose(sgu_w, (0, 2, 1, 3)).reshape(n_layers, CHUNK, HEADS * CHUNK)
    sgu_bias = jnp.repeat(jnp.transpose(sgu_b, (0, 2, 1)), HEAD_DIM, axis=-1)
    return {
        "g_pre_mix": row(g_pre_mix), "g_post_mix": row(g_post_mix),
        "g_pre_ffn": row(g_pre_ffn), "g_post_ffn": row(g_post_ffn),
        "w_in": w_in.astype(_BF16), "conv_a_w": conv_a_w, "conv_a_b": row(conv_a_b),
        "ln_a_g": row(ln_a_g), "ln_a_b": row(ln_a_b),
        "pool_w": pool_bd.astype(_BF16), "pool_scale": row(pool_scale),
        "ln_c_g": row(ln_c_g), "ln_c_b": row(ln_c_b),
        "sgu_w": sgu_cat, "sgu_bias": sgu_bias,
        "sgu_w00": row(jnp.repeat(sgu_w[:, :, 0, 0], HEAD_DIM, axis=-1)),
        "sgu_b0": row(jnp.repeat(sgu_b[:, :, 0], HEAD_DIM, axis=-1)),
        "conv_d_w": conv_d_w,
        "w_out": w_out.astype(_BF16), "w_gate": w_gate.astype(_BF16),
        "w_up": w_up.astype(_BF16), "w_down": w_down.astype(_BF16),
    }


def kernel(x_prompt, x_sample, c_prompt, c_sample, state_conv_a, state_pool_b, state_conv_d, w_ada, b_ada, g_pre_mix, g_post_mix, w_in, conv_a_w, conv_a_b, ln_a_g, ln_a_b, pool_w, pool_scale, ln_c_g, ln_c_b, sgu_w, sgu_b, conv_d_w, w_out, g_pre_ffn, g_post_ffn, w_gate, w_up, w_down):
    n_layers = w_in.shape[0]
    n_sample = x_sample.shape[0]
    assert x_prompt.shape[1] % TOKEN_TILE == 0 and TOKEN_TILE % CHUNK == 0
    assert x_sample.shape[1] == 1 and n_sample % x_prompt.shape[0] == 0

    p = _prepare_params(g_pre_mix, g_post_mix, w_in, conv_a_w, conv_a_b, ln_a_g, ln_a_b, pool_w,
                        pool_scale, ln_c_g, ln_c_b, sgu_w, sgu_b, conv_d_w, w_out, g_pre_ffn,
                        g_post_ffn, w_gate, w_up, w_down)
    mod = _ada_call(jnp.concatenate([c_sample, c_prompt], axis=0), w_ada, b_ada)

    sta = jnp.transpose(state_conv_a, (0, 2, 1, 3))
    stb = jnp.transpose(state_pool_b, (0, 2, 1, 3))
    std = jnp.transpose(state_conv_d, (0, 2, 1, 3))

    xp = x_prompt
    xs = x_sample.reshape(n_sample, D_MODEL)
    outs = [[] for _ in range(7)]
    for layer in range(n_layers):
        xp, a_p, b_p, d_p = _prompt_layer(layer, xp, mod, p)
        xs, a_s, b_s, d_s, v_s = _sample_layer(layer, xs, mod, sta, stb, std, p)
        for acc, val in zip(outs, (a_p, b_p, d_p, a_s, b_s, d_s, v_s)):
            acc.append(val)

    na_p, nb_p, nd_p = (jnp.stack(o) for o in outs[:3])
    na_s, nb_s, nd_s = (jnp.transpose(jnp.stack(o), (0, 2, 1, 3)) for o in outs[3:6])
    nv_s = jnp.stack(outs[6])[:, :, None, :]
    return (xp, xs.reshape(n_sample, 1, D_MODEL), na_p, nb_p, nd_p, na_s, nb_s, nd_s, nv_s)
```

```python
import functools

import jax
import jax.numpy as jnp
from jax import lax
from jax.experimental import pallas as pl
from jax.experimental.pallas import tpu as pltpu

D_MODEL = 1024
GROUP_W = 256
HEADS = 4
HEAD_DIM = GROUP_W // HEADS
HEAD_SHIFT = HEAD_DIM.bit_length() - 1
IN_WIDTH = 8 * GROUP_W
CONV_A = 31
CONV_D = 3
POOL_WINDOWS = (2, 4, 8, 16)
POOL_BUF = 15
CHUNK = 128
D_FF = 2816
N_MOD = 6
RMS_EPS = 1e-6
LN_EPS = 1e-5

SUBLANES = 8
LANES = 128

TOKEN_TILE = 256
CONV_ROWS = 64
N_SLABS = GROUP_W // LANES
HIST_A = 32
HIST_B = 16
HIST_D = 8
ADA_TILE_N = 1024
VMEM_LIMIT_BYTES = 56 * 1024 * 1024

_BF16 = jnp.bfloat16
_F32 = jnp.float32


def _dot(a, w):
    return jnp.dot(a.astype(_BF16), w, preferred_element_type=_F32)


def _rms_unit(x):
    return x * lax.rsqrt(jnp.mean(x * x, axis=-1, keepdims=True) + RMS_EPS)


def _ln(x, g, b):
    mu = jnp.mean(x, axis=-1, keepdims=True)
    xc = x - mu
    var = jnp.mean(xc * xc, axis=-1, keepdims=True)
    return xc * lax.rsqrt(var + LN_EPS) * g + b


def _sigmoid(x):
    return 0.5 + 0.5 * jnp.tanh(0.5 * x)


def _silu(x):
    hx = 0.5 * x
    return hx + hx * jnp.tanh(hx)


def _slab(v, s):
    return v[:, s * LANES:(s + 1) * LANES]


def _even_odd(n_rows, start=0):
    return [pl.ds(start + p, n_rows // 2, stride=2) for p in range(2)]


def _split_proj(proj):
    return [proj[:, i * GROUP_W:(i + 1) * GROUP_W] for i in range(8)]


def _out_and_ffn(x, mods, ys, g_post_mix, g_pre_ffn, g_post_ffn, wout_ref, wg_ref, wu_ref, wd_ref):
    _, _, gt1, sh2, sc2, gt2 = mods
    y = jnp.concatenate([v.astype(_BF16) for v in ys], axis=-1)
    m = jnp.dot(y, wout_ref[...], preferred_element_type=_F32)
    x = x + _rms_unit(m) * (gt1 * g_post_mix)
    h = (_rms_unit(x) * (g_pre_ffn * (1.0 + sc2)) + sh2).astype(_BF16)
    gate = jnp.dot(h, wg_ref[...], preferred_element_type=_F32)
    up = jnp.dot(h, wu_ref[...], preferred_element_type=_F32)
    f = _dot(_silu(gate) * up, wd_ref[...])
    return x + _rms_unit(f) * (gt2 * g_post_ffn)


def _ada_kernel(c_ref, w_ref, b_ref, o_ref):
    s = _silu(c_ref[...])
    o_ref[...] = _dot(s, w_ref[...].astype(_BF16)) + b_ref[...]


def _ada_call(c_all, w_ada, b_ada):
    n_layers = w_ada.shape[0]
    rows = c_all.shape[0]
    n_out = N_MOD * D_MODEL
    return pl.pallas_call(
        _ada_kernel,
        grid=(n_layers, n_out // ADA_TILE_N),
        in_specs=[
            pl.BlockSpec((rows, D_MODEL), lambda l, j: (0, 0)),
            pl.BlockSpec((None, D_MODEL, ADA_TILE_N), lambda l, j: (l, 0, j)),
            pl.BlockSpec((None, 1, ADA_TILE_N), lambda l, j: (l, 0, j)),
        ],
        out_specs=pl.BlockSpec((None, rows, ADA_TILE_N), lambda l, j: (l, 0, j)),
        out_shape=jax.ShapeDtypeStruct((n_layers, rows, n_out), _F32),
        compiler_params=pltpu.CompilerParams(dimension_semantics=("arbitrary", "arbitrary")),
        name="ada_mod",
    )(c_all, w_ada, b_ada.reshape(n_layers, 1, n_out))


def _prompt_kernel(x_ref, mod_ref, gpm_ref, gqm_ref, gpf_ref, gqf_ref, win_ref, caw_ref, cab_ref,
                   lag_ref, lab_ref, pw_ref, ps_ref, lcg_ref, lcb_ref, sw_ref, sbias_ref, cdw_ref,
                   wout_ref, wg_ref, wu_ref, wd_ref,
                   xo_ref, na_ref, nb_ref, nd_ref, sa, sb, sd, oa, ob, od):
    tm = TOKEN_TILE
    half = tm // 2
    b = pl.program_id(0)
    t = pl.program_id(1)

    @pl.when(t == 0)
    def _():
        sa[:, 0:HIST_A, :] = jnp.zeros((N_SLABS, HIST_A, LANES), _F32)
        sb[:, 0:HIST_B, :] = jnp.zeros((N_SLABS, HIST_B, LANES), _F32)
        sd[:, 0:HIST_D, :] = jnp.zeros((N_SLABS, HIST_D, LANES), _F32)

    x = x_ref[...]
    mod = mod_ref[pl.ds(b, 1), :]
    mods = [mod[:, i * D_MODEL:(i + 1) * D_MODEL] for i in range(N_MOD)]
    sh1, sc1 = mods[0], mods[1]

    h = _rms_unit(x) * (gpm_ref[...] * (1.0 + sc1)) + sh1
    proj = _dot(h, win_ref[...])
    a_val, a_gate, b_in, c_u, c_v, d_b, d_c, d_h = _split_proj(proj)

    za = a_val * _sigmoid(a_gate)
    for s in range(N_SLABS):
        sa[s, HIST_A:HIST_A + tm, :] = _slab(za, s)
        lanes = slice(s * LANES, (s + 1) * LANES)
        for p in range(2):
            for i0 in range(0, half, CONV_ROWS):
                acc = jnp.broadcast_to(cab_ref[:, lanes], (CONV_ROWS, LANES))
                first = HIST_A - (CONV_A - 1) + p + 2 * i0
                for k in range(CONV_A):
                    acc = acc + caw_ref[k:k + 1, lanes] * sa[s, pl.ds(first + k, CONV_ROWS, stride=2), :]
                oa[s, pl.ds(p + 2 * i0, CONV_ROWS, stride=2), :] = acc
    conv = jnp.concatenate([oa[s] for s in range(N_SLABS)], axis=-1)
    ya = _silu(_ln(conv, lag_ref[...], lab_ref[...]))

    row2 = 2 * lax.broadcasted_iota(jnp.int32, (half, LANES), 0)
    low_half = lax.broadcasted_iota(jnp.int32, (half, LANES), 1) < HEAD_DIM
    for s in range(N_SLABS):
        sb[s, HIST_B:HIST_B + tm, :] = _slab(b_in, s)
        w_lo, w_hi = POOL_WINDOWS[2 * s], POOL_WINDOWS[2 * s + 1]
        for p, rows in enumerate(_even_odd(tm)):
            z = sb[s, pl.ds(HIST_B + p, half, stride=2), :]
            s_lo = z
            for j in range(1, w_lo):
                s_lo = s_lo + sb[s, pl.ds(HIST_B + p - j, half, stride=2), :]
            s_hi = s_lo
            for j in range(w_lo, w_hi):
                s_hi = s_hi + sb[s, pl.ds(HIST_B + p - j, half, stride=2), :]
            pos1 = t * tm + (p + 1) + row2
            cnt = jnp.where(low_half, jnp.minimum(pos1, w_lo), jnp.minimum(pos1, w_hi)).astype(_F32)
            ob[s, rows, :] = jnp.where(low_half, s_lo, s_hi) / cnt - z
    diff = jnp.concatenate([ob[s] for s in range(N_SLABS)], axis=-1)
    yb = _dot(diff, pw_ref[...]) * ps_ref[...]

    vn = _ln(c_v, lcg_ref[...], lcb_ref[...])
    row_i = lax.broadcasted_iota(jnp.int32, (CHUNK, HEADS * CHUNK), 0)
    col_j = lax.broadcasted_iota(jnp.int32, (CHUNK, HEADS * CHUNK), 1) & (CHUNK - 1)
    w_tril = jnp.where(col_j <= row_i, sw_ref[...], 0.0).astype(_BF16)
    head_of_lane = lax.broadcasted_iota(jnp.int32, (CHUNK, GROUP_W), 1) >> HEAD_SHIFT
    gated = []
    for ci in range(tm // CHUNK):
        v = vn[ci * CHUNK:(ci + 1) * CHUNK, :]
        stacked = jnp.concatenate(
            [jnp.where(head_of_lane == hd, v, 0.0).astype(_BF16) for hd in range(HEADS)], axis=0)
        gated.append(jnp.dot(w_tril, stacked, preferred_element_type=_F32) + sbias_ref[...])
    yc = c_u * jnp.concatenate(gated, axis=0)

    q = d_c * d_h
    for s in range(N_SLABS):
        sd[s, HIST_D:HIST_D + tm, :] = _slab(q, s)
        lanes = slice(s * LANES, (s + 1) * LANES)
        for p, rows in enumerate(_even_odd(tm)):
            od[s, rows, :] = (cdw_ref[0:1, lanes] * sd[s, pl.ds(HIST_D + p - 2, half, stride=2), :]
                              + cdw_ref[1:2, lanes] * sd[s, pl.ds(HIST_D + p - 1, half, stride=2), :]
                              + cdw_ref[2:3, lanes] * sd[s, pl.ds(HIST_D + p, half, stride=2), :])
    yd = d_b * jnp.concatenate([od[s] for s in range(N_SLABS)], axis=-1)

    @pl.when(t == pl.num_programs(1) - 1)
    def _():
        for s in range(N_SLABS):
            lanes = slice(s * LANES, (s + 1) * LANES)
            na_ref[:, lanes] = sa[s, pl.ds(HIST_A + tm - (CONV_A - 1), CONV_A - 1), :]
            nb_ref[:, lanes] = sb[s, pl.ds(HIST_B + tm - POOL_BUF, POOL_BUF), :]
            nd_ref[:, lanes] = sd[s, pl.ds(HIST_D + tm - (CONV_D - 1), CONV_D - 1), :]

    sa[:, 0:HIST_A, :] = sa[:, tm:tm + HIST_A, :]
    sb[:, 0:HIST_B, :] = sb[:, tm:tm + HIST_B, :]
    sd[:, 0:HIST_D, :] = sd[:, tm:tm + HIST_D, :]

    xo_ref[...] = _out_and_ffn(x, mods, (ya, yb, yc, yd), gqm_ref[...], gpf_ref[...], gqf_ref[...],
                               wout_ref, wg_ref, wu_ref, wd_ref)


def _layer_spec(shape, n_grid, single_buffer=False):
    zeros = (0,) * len(shape)
    kwargs = {"pipeline_mode": pl.Buffered(1)} if single_buffer else {}

    def make(layer):
        if n_grid == 2:
            return pl.BlockSpec((None,) + tuple(shape), lambda b, t: (layer,) + zeros, **kwargs)
        return pl.BlockSpec((None,) + tuple(shape), lambda i: (layer,) + zeros, **kwargs)
    return make


def _prompt_layer(layer, x, mod, p):
    bsz, seq, _ = x.shape
    tm = TOKEN_TILE
    n_sample = mod.shape[1] - bsz

    def vec(n):
        return _layer_spec((1, n), 2)(layer)

    def mat(shape):
        return _layer_spec(shape, 2, single_buffer=True)(layer)

    in_specs = [
        pl.BlockSpec((None, tm, D_MODEL), lambda b, t: (b, t, 0)),
        pl.BlockSpec((None, bsz, N_MOD * D_MODEL), lambda b, t: (layer, n_sample // bsz, 0)),
        vec(D_MODEL), vec(D_MODEL), vec(D_MODEL), vec(D_MODEL),
        mat((D_MODEL, IN_WIDTH)),
        _layer_spec((CONV_A, GROUP_W), 2)(layer), vec(GROUP_W), vec(GROUP_W), vec(GROUP_W),
        mat((GROUP_W, GROUP_W)), vec(GROUP_W), vec(GROUP_W), vec(GROUP_W),
        _layer_spec((CHUNK, HEADS * CHUNK), 2)(layer), _layer_spec((CHUNK, GROUP_W), 2)(layer),
        _layer_spec((CONV_D, GROUP_W), 2)(layer),
        mat((D_MODEL, D_MODEL)), mat((D_MODEL, D_FF)), mat((D_MODEL, D_FF)), mat((D_FF, D_MODEL)),
    ]
    out_specs = [
        pl.BlockSpec((None, tm, D_MODEL), lambda b, t: (b, t, 0)),
        pl.BlockSpec((None, CONV_A - 1, GROUP_W), lambda b, t: (b, 0, 0)),
        pl.BlockSpec((None, POOL_BUF, GROUP_W), lambda b, t: (b, 0, 0)),
        pl.BlockSpec((None, CONV_D - 1, GROUP_W), lambda b, t: (b, 0, 0)),
    ]
    out_shape = [
        jax.ShapeDtypeStruct((bsz, seq, D_MODEL), _F32),
        jax.ShapeDtypeStruct((bsz, CONV_A - 1, GROUP_W), _F32),
        jax.ShapeDtypeStruct((bsz, POOL_BUF, GROUP_W), _F32),
        jax.ShapeDtypeStruct((bsz, CONV_D - 1, GROUP_W), _F32),
    ]
    return pl.pallas_call(
        _prompt_kernel,
        grid=(bsz, seq // tm),
        in_specs=in_specs,
        out_specs=out_specs,
        out_shape=out_shape,
        scratch_shapes=[
            pltpu.VMEM((N_SLABS, HIST_A + tm, LANES), _F32),
            pltpu.VMEM((N_SLABS, HIST_B + tm, LANES), _F32),
            pltpu.VMEM((N_SLABS, HIST_D + tm, LANES), _F32),
            pltpu.VMEM((N_SLABS, tm, LANES), _F32),
            pltpu.VMEM((N_SLABS, tm, LANES), _F32),
            pltpu.VMEM((N_SLABS, tm, LANES), _F32),
        ],
        compiler_params=pltpu.CompilerParams(
            dimension_semantics=("arbitrary", "arbitrary"), vmem_limit_bytes=VMEM_LIMIT_BYTES),
        name=f"prompt_layer{layer}",
    )(x, mod, p["g_pre_mix"], p["g_post_mix"], p["g_pre_ffn"], p["g_post_ffn"], p["w_in"],
      p["conv_a_w"], p["conv_a_b"], p["ln_a_g"], p["ln_a_b"], p["pool_w"], p["pool_scale"],
      p["ln_c_g"], p["ln_c_b"], p["sgu_w"], p["sgu_bias"], p["conv_d_w"],
      p["w_out"], p["w_gate"], p["w_up"], p["w_down"])


def _sample_kernel(x_ref, mod_ref, sta_ref, stb_ref, std_ref, gpm_ref, gqm_ref, gpf_ref, gqf_ref,
                   win_ref, caw_ref, cab_ref, lag_ref, lab_ref, pw_ref, ps_ref, lcg_ref, lcb_ref,
                   sw0_ref, sb0_ref, cdw_ref, wout_ref, wg_ref, wu_ref, wd_ref,
                   xo_ref, na_ref, nb_ref, nd_ref, nv_ref):
    x = x_ref[...]
    mods = [mod_ref[:, i * D_MODEL:(i + 1) * D_MODEL] for i in range(N_MOD)]
    sh1, sc1 = mods[0], mods[1]
    h = _rms_unit(x) * (gpm_ref[...] * (1.0 + sc1)) + sh1
    proj = _dot(h, win_ref[...])
    a_val, a_gate, b_in, c_u, c_v, d_b, d_c, d_h = _split_proj(proj)

    za = a_val * _sigmoid(a_gate)
    acc = cab_ref[...] + caw_ref[CONV_A - 1:CONV_A, :] * za
    for k in range(CONV_A - 1):
        acc = acc + caw_ref[k:k + 1, :] * sta_ref[k]
    ya = _silu(_ln(acc, lag_ref[...], lab_ref[...]))
    for k in range(CONV_A - 2):
        na_ref[k] = sta_ref[k + 1]
    na_ref[CONV_A - 2] = za

    lane = lax.broadcasted_iota(jnp.int32, b_in.shape, 1)
    window_sum = b_in
    pooled = jnp.zeros_like(b_in)
    j = 1
    for g, w in enumerate(POOL_WINDOWS):
        while j < w:
            window_sum = window_sum + stb_ref[POOL_BUF - j]
            j += 1
        pooled = jnp.where((lane >> HEAD_SHIFT) == g, window_sum / float(w), pooled)
    yb = _dot(pooled - b_in, pw_ref[...]) * ps_ref[...]
    for k in range(POOL_BUF - 1):
        nb_ref[k] = stb_ref[k + 1]
    nb_ref[POOL_BUF - 1] = b_in

    vn = _ln(c_v, lcg_ref[...], lcb_ref[...])
    nv_ref[...] = vn
    yc = c_u * (sw0_ref[...] * vn + sb0_ref[...])

    q = d_c * d_h
    yd = d_b * (cdw_ref[0:1, :] * std_ref[0] + cdw_ref[1:2, :] * std_ref[1] + cdw_ref[2:3, :] * q)
    nd_ref[0] = std_ref[1]
    nd_ref[1] = q

    xo_ref[...] = _out_and_ffn(x, mods, (ya, yb, yc, yd), gqm_ref[...], gpf_ref[...], gqf_ref[...],
                               wout_ref, wg_ref, wu_ref, wd_ref)


def _sample_layer(layer, x, mod, sta, stb, std, p):
    n = x.shape[0]

    def vec(width):
        return _layer_spec((1, width), 1)(layer)

    def mat(shape):
        return _layer_spec(shape, 1, single_buffer=True)(layer)

    def state(rows):
        return pl.BlockSpec((None, rows, n, GROUP_W), lambda i: (layer, 0, 0, 0))

    in_specs = [
        pl.BlockSpec((n, D_MODEL), lambda i: (0, 0)),
        pl.BlockSpec((None, n, N_MOD * D_MODEL), lambda i: (layer, 0, 0)),
        state(CONV_A - 1), state(POOL_BUF), state(CONV_D - 1),
        vec(D_MODEL), vec(D_MODEL), vec(D_MODEL), vec(D_MODEL),
        mat((D_MODEL, IN_WIDTH)),
        _layer_spec((CONV_A, GROUP_W), 1)(layer), vec(GROUP_W), vec(GROUP_W), vec(GROUP_W),
        mat((GROUP_W, GROUP_W)), vec(GROUP_W), vec(GROUP_W), vec(GROUP_W),
        vec(GROUP_W), vec(GROUP_W),
        _layer_spec((CONV_D, GROUP_W), 1)(layer),
        mat((D_MODEL, D_MODEL)), mat((D_MODEL, D_FF)), mat((D_MODEL, D_FF)), mat((D_FF, D_MODEL)),
    ]
    out_specs = [
        pl.BlockSpec((n, D_MODEL), lambda i: (0, 0)),
        pl.BlockSpec((CONV_A - 1, n, GROUP_W), lambda i: (0, 0, 0)),
        pl.BlockSpec((POOL_BUF, n, GROUP_W), lambda i: (0, 0, 0)),
        pl.BlockSpec((CONV_D - 1, n, GROUP_W), lambda i: (0, 0, 0)),
        pl.BlockSpec((n, GROUP_W), lambda i: (0, 0)),
    ]
    out_shape = [
        jax.ShapeDtypeStruct((n, D_MODEL), _F32),
        jax.ShapeDtypeStruct((CONV_A - 1, n, GROUP_W), _F32),
        jax.ShapeDtypeStruct((POOL_BUF, n, GROUP_W), _F32),
        jax.ShapeDtypeStruct((CONV_D - 1, n, GROUP_W), _F32),
        jax.ShapeDtypeStruct((n, GROUP_W), _F32),
    ]
    return pl.pallas_call(
        _sample_kernel,
        grid=(1,),
        in_specs=in_specs,
        out_specs=out_specs,
        out_shape=out_shape,
        compiler_params=pltpu.CompilerParams(
            dimension_semantics=("arbitrary",), vmem_limit_bytes=VMEM_LIMIT_BYTES),
        name=f"sample_layer{layer}",
    )(x, mod, sta, stb, std, p["g_pre_mix"], p["g_post_mix"], p["g_pre_ffn"], p["g_post_ffn"],
      p["w_in"], p["conv_a_w"], p["conv_a_b"], p["ln_a_g"], p["ln_a_b"], p["pool_w"],
      p["pool_scale"], p["ln_c_g"], p["ln_c_b"], p["sgu_w00"], p["sgu_b0"], p["conv_d_w"],
      p["w_out"], p["w_gate"], p["w_up"], p["w_down"])


def _prepare_params(g_pre_mix, g_post_mix, w_in, conv_a_w, conv_a_b, ln_a_g, ln_a_b, pool_w, pool_scale,
                    ln_c_g, ln_c_b, sgu_w, sgu_b, conv_d_w, w_out, g_pre_ffn, g_post_ffn,
                    w_gate, w_up, w_down):
    n_layers = w_in.shape[0]

    def row(v):
        return v.reshape(n_layers, 1, v.shape[-1])

    n_groups = len(POOL_WINDOWS)
    eye = jnp.eye(n_groups, dtype=pool_w.dtype)
    pool_bd = jnp.einsum("lgcd,gh->lgchd", pool_w, eye).reshape(n_layers, GROUP_W, GROUP_W)
    sgu_cat = jnp.transpose(sgu_w, (0, 2, 1, 3)).reshape(n_layers, CHUNK, HEADS * CHUNK)
    sgu_bias = jnp.repeat(jnp.transpose(sgu_b, (0, 2, 1)), HEAD_DIM, axis=-1)
    return {
        "g_pre_mix": row(g_pre_mix), "g_post_mix": row(g_post_mix),
        "g_pre_ffn": row(g_pre_ffn), "g_post_ffn": row(g_post_ffn),
        "w_in": w_in.astype(_BF16), "conv_a_w": conv_a_w, "conv_a_b": row(conv_a_b),
        "ln_a_g": row(ln_a_g), "ln_a_b": row(ln_a_b),
        "pool_w": pool_bd.astype(_BF16), "pool_scale": row(pool_scale),
        "ln_c_g": row(ln_c_g), "ln_c_b": row(ln_c_b),
        "sgu_w": sgu_cat, "sgu_bias": sgu_bias,
        "sgu_w00": row(jnp.repeat(sgu_w[:, :, 0, 0], HEAD_DIM, axis=-1)),
        "sgu_b0": row(jnp.repeat(sgu_b[:, :, 0], HEAD_DIM, axis=-1)),
        "conv_d_w": conv_d_w,
        "w_out": w_out.astype(_BF16), "w_gate": w_gate.astype(_BF16),
        "w_up": w_up.astype(_BF16), "w_down": w_down.astype(_BF16),
    }


def kernel(x_prompt, x_sample, c_prompt, c_sample, state_conv_a, state_pool_b, state_conv_d, w_ada, b_ada, g_pre_mix, g_post_mix, w_in, conv_a_w, conv_a_b, ln_a_g, ln_a_b, pool_w, pool_scale, ln_c_g, ln_c_b, sgu_w, sgu_b, conv_d_w, w_out, g_pre_ffn, g_post_ffn, w_gate, w_up, w_down):
    n_layers = w_in.shape[0]
    n_sample = x_sample.shape[0]
    assert x_prompt.shape[1] % TOKEN_TILE == 0 and TOKEN_TILE % CHUNK == 0
    assert x_sample.shape[1] == 1 and n_sample % x_prompt.shape[0] == 0

    p = _prepare_params(g_pre_mix, g_post_mix, w_in, conv_a_w, conv_a_b, ln_a_g, ln_a_b, pool_w,
                        pool_scale, ln_c_g, ln_c_b, sgu_w, sgu_b, conv_d_w, w_out, g_pre_ffn,
                        g_post_ffn, w_gate, w_up, w_down)
    mod = _ada_call(jnp.concatenate([c_sample, c_prompt], axis=0), w_ada, b_ada)

    sta = jnp.transpose(state_conv_a, (0, 2, 1, 3))
    stb = jnp.transpose(state_pool_b, (0, 2, 1, 3))
    std = jnp.transpose(state_conv_d, (0, 2, 1, 3))

    xp = x_prompt
    xs = x_sample.reshape(n_sample, D_MODEL)
    outs = [[] for _ in range(7)]
    for layer in range(n_layers):
        xp, a_p, b_p, d_p = _prompt_layer(layer, xp, mod, p)
        xs, a_s, b_s, d_s, v_s = _sample_layer(layer, xs, mod, sta, stb, std, p)
        for acc, val in zip(outs, (a_p, b_p, d_p, a_s, b_s, d_s, v_s)):
            acc.append(val)

    na_p, nb_p, nd_p = (jnp.stack(o) for o in outs[:3])
    na_s, nb_s, nd_s = (jnp.transpose(jnp.stack(o), (0, 2, 1, 3)) for o in outs[3:6])
    nv_s = jnp.stack(outs[6])[:, :, None, :]
    return (xp, xs.reshape(n_sample, 1, D_MODEL), na_p, nb_p, nd_p, na_s, nb_s, nd_s, nv_s)
```

```python
import functools

import jax
import jax.numpy as jnp
from jax import lax
from jax.experimental import pallas as pl
from jax.experimental.pallas import tpu as pltpu

D_MODEL = 1024
GROUP_W = 256
HEADS = 4
HEAD_DIM = GROUP_W // HEADS
HEAD_SHIFT = HEAD_DIM.bit_length() - 1
IN_WIDTH = 8 * GROUP_W
CONV_A = 31
CONV_D = 3
POOL_WINDOWS = (2, 4, 8, 16)
POOL_BUF = 15
CHUNK = 128
D_FF = 2816
N_MOD = 6
RMS_EPS = 1e-6
LN_EPS = 1e-5

SUBLANES = 8
LANES = 128

TOKEN_TILE = 512
CONV_ROWS = 64
FF_SPLIT = 1536
FFN_OUT_ROWS = 256
N_SLABS = GROUP_W // LANES
HIST_A = 32
HIST_B = 16
HIST_D = 8
ADA_TILE_N = 1024
VMEM_LIMIT_BYTES = 56 * 1024 * 1024

_BF16 = jnp.bfloat16
_F32 = jnp.float32


def _dot(a, w):
    return jnp.dot(a.astype(_BF16), w, preferred_element_type=_F32)


def _rms_unit(x):
    return x * lax.rsqrt(jnp.mean(x * x, axis=-1, keepdims=True) + RMS_EPS)


def _ln(x, g, b):
    mu = jnp.mean(x, axis=-1, keepdims=True)
    xc = x - mu
    var = jnp.mean(xc * xc, axis=-1, keepdims=True)
    return xc * lax.rsqrt(var + LN_EPS) * g + b


def _sigmoid(x):
    return 0.5 + 0.5 * jnp.tanh(0.5 * x)


def _silu(x):
    hx = 0.5 * x
    return hx + hx * jnp.tanh(hx)


def _slab(v, s):
    return v[:, s * LANES:(s + 1) * LANES]


def _even_odd(n_rows, start=0):
    return [pl.ds(start + p, n_rows // 2, stride=2) for p in range(2)]


def _split_proj(proj):
    return [proj[:, i * GROUP_W:(i + 1) * GROUP_W] for i in range(8)]


def _out_and_ffn(x, mods, ys, g_post_mix, g_pre_ffn, g_post_ffn, wout_ref, wg_ref, wu_ref, wd_ref):
    _, _, gt1, sh2, sc2, gt2 = mods
    y = jnp.concatenate([v.astype(_BF16) for v in ys], axis=-1)
    m = jnp.dot(y, wout_ref[...], preferred_element_type=_F32)
    x = x + _rms_unit(m) * (gt1 * g_post_mix)
    h = (_rms_unit(x) * (g_pre_ffn * (1.0 + sc2)) + sh2).astype(_BF16)
    gate = jnp.dot(h, wg_ref[...], preferred_element_type=_F32)
    up = jnp.dot(h, wu_ref[...], preferred_element_type=_F32)
    f = _dot(_silu(gate) * up, wd_ref[...])
    return x + _rms_unit(f) * (gt2 * g_post_ffn)


def _ada_kernel(c_ref, w_ref, b_ref, o_ref):
    s = _silu(c_ref[...])
    o_ref[...] = _dot(s, w_ref[...].astype(_BF16)) + b_ref[...]


def _ada_call(c_all, w_ada, b_ada):
    n_layers = w_ada.shape[0]
    rows = c_all.shape[0]
    n_out = N_MOD * D_MODEL
    return pl.pallas_call(
        _ada_kernel,
        grid=(n_layers, n_out // ADA_TILE_N),
        in_specs=[
            pl.BlockSpec((rows, D_MODEL), lambda l, j: (0, 0)),
            pl.BlockSpec((None, D_MODEL, ADA_TILE_N), lambda l, j: (l, 0, j)),
            pl.BlockSpec((None, 1, ADA_TILE_N), lambda l, j: (l, 0, j)),
        ],
        out_specs=pl.BlockSpec((None, rows, ADA_TILE_N), lambda l, j: (l, 0, j)),
        out_shape=jax.ShapeDtypeStruct((n_layers, rows, n_out), _F32),
        compiler_params=pltpu.CompilerParams(dimension_semantics=("arbitrary", "arbitrary")),
        name="ada_mod",
    )(c_all, w_ada, b_ada.reshape(n_layers, 1, n_out))


def _prompt_kernel(tiles_per_seq, n_seq,
                   x_ref, mod_ref, gpm_ref, gqm_ref, gpf_ref, gqf_ref, win_ref, caw_ref, cab_ref,
                   lag_ref, lab_ref, pw_ref, ps_ref, lcg_ref, lcb_ref, sw_ref, sbias_ref, cdw_ref,
                   wout_ref, wg_ref, wu_ref, wd_ref,
                   xo_ref, na_ref, nb_ref, nd_ref, sa, sb, sd, oa, ob, od, x1_s, h2_s):
    tm = TOKEN_TILE
    half = tm // 2
    g = pl.program_id(0)
    t = lax.rem(g, tiles_per_seq)
    b_mix = jnp.minimum(lax.div(g, tiles_per_seq), n_seq - 1)
    b_ffn = lax.div(jnp.maximum(g - 1, 0), tiles_per_seq)

    @pl.when(g == 0)
    def _():
        x1_s[...] = jnp.zeros(x1_s.shape, _F32)
        h2_s[...] = jnp.zeros(h2_s.shape, _BF16)

    @pl.when(t == 0)
    def _():
        sa[:, 0:HIST_A, :] = jnp.zeros((N_SLABS, HIST_A, LANES), _F32)
        sb[:, 0:HIST_B, :] = jnp.zeros((N_SLABS, HIST_B, LANES), _F32)
        sd[:, 0:HIST_D, :] = jnp.zeros((N_SLABS, HIST_D, LANES), _F32)

    mod_m = mod_ref[pl.ds(b_mix, 1), :]
    sh1, sc1, gt1, sh2, sc2 = [mod_m[:, i * D_MODEL:(i + 1) * D_MODEL] for i in range(5)]
    gt2 = mod_ref[pl.ds(b_ffn, 1), 5 * D_MODEL:6 * D_MODEL]

    x = x_ref[...]
    h = (_rms_unit(x) * (gpm_ref[...] * (1.0 + sc1)) + sh1).astype(_BF16)

    def swiglu(cols):
        h2 = h2_s[...]
        gate = jnp.dot(h2, wg_ref[:, cols], preferred_element_type=_F32)
        up = jnp.dot(h2, wu_ref[:, cols], preferred_element_type=_F32)
        return (_silu(gate) * up).astype(_BF16)

    xo_ref[...] = x1_s[...]

    ff_a, ff_b = slice(0, FF_SPLIT), slice(FF_SPLIT, D_FF)
    act_a = swiglu(ff_a)

    proj = jnp.dot(h, win_ref[...], preferred_element_type=_F32)
    a_val, a_gate, b_in, c_u, c_v, d_b, d_c, d_h = _split_proj(proj)

    f_a = jnp.dot(act_a, wd_ref[ff_a, :], preferred_element_type=_F32)

    za = a_val * _sigmoid(a_gate)
    for s in range(N_SLABS):
        sa[s, HIST_A:HIST_A + tm, :] = _slab(za, s)
        lanes = slice(s * LANES, (s + 1) * LANES)
        for p in range(2):
            for i0 in range(0, half, CONV_ROWS):
                acc = jnp.broadcast_to(cab_ref[:, lanes], (CONV_ROWS, LANES))
                first = HIST_A - (CONV_A - 1) + p + 2 * i0
                for k in range(CONV_A):
                    acc = acc + caw_ref[k:k + 1, lanes] * sa[s, pl.ds(first + k, CONV_ROWS, stride=2), :]
                oa[s, pl.ds(p + 2 * i0, CONV_ROWS, stride=2), :] = acc
    conv = jnp.concatenate([oa[s] for s in range(N_SLABS)], axis=-1)
    ya = _silu(_ln(conv, lag_ref[...], lab_ref[...]))

    row2 = 2 * lax.broadcasted_iota(jnp.int32, (half, LANES), 0)
    low_half = lax.broadcasted_iota(jnp.int32, (half, LANES), 1) < HEAD_DIM
    for s in range(N_SLABS):
        sb[s, HIST_B:HIST_B + tm, :] = _slab(b_in, s)
        w_lo, w_hi = POOL_WINDOWS[2 * s], POOL_WINDOWS[2 * s + 1]
        for p, rows in enumerate(_even_odd(tm)):
            z = sb[s, pl.ds(HIST_B + p, half, stride=2), :]
            s_lo = z
            for j in range(1, w_lo):
                s_lo = s_lo + sb[s, pl.ds(HIST_B + p - j, half, stride=2), :]
            s_hi = s_lo
            for j in range(w_lo, w_hi):
                s_hi = s_hi + sb[s, pl.ds(HIST_B + p - j, half, stride=2), :]
            pos1 = t * tm + (p + 1) + row2
            cnt = jnp.where(low_half, jnp.minimum(pos1, w_lo), jnp.minimum(pos1, w_hi)).astype(_F32)
            ob[s, rows, :] = jnp.where(low_half, s_lo, s_hi) / cnt - z
    diff = jnp.concatenate([ob[s] for s in range(N_SLABS)], axis=-1)
    yb = _dot(diff, pw_ref[...]) * ps_ref[...]

    vn = _ln(c_v, lcg_ref[...], lcb_ref[...])
    row_i = lax.broadcasted_iota(jnp.int32, (CHUNK, HEADS * CHUNK), 0)
    col_j = lax.broadcasted_iota(jnp.int32, (CHUNK, HEADS * CHUNK), 1) & (CHUNK - 1)
    w_tril = jnp.where(col_j <= row_i, sw_ref[...], 0.0).astype(_BF16)
    head_of_lane = lax.broadcasted_iota(jnp.int32, (CHUNK, GROUP_W), 1) >> HEAD_SHIFT
    gated = []
    for ci in range(tm // CHUNK):
        v = vn[ci * CHUNK:(ci + 1) * CHUNK, :]
        stacked = jnp.concatenate(
            [jnp.where(head_of_lane == hd, v, 0.0).astype(_BF16) for hd in range(HEADS)], axis=0)
        gated.append(jnp.dot(w_tril, stacked, preferred_element_type=_F32) + sbias_ref[...])
    yc = c_u * jnp.concatenate(gated, axis=0)

    q = d_c * d_h
    for s in range(N_SLABS):
        sd[s, HIST_D:HIST_D + tm, :] = _slab(q, s)
        lanes = slice(s * LANES, (s + 1) * LANES)
        for p, rows in enumerate(_even_odd(tm)):
            od[s, rows, :] = (cdw_ref[0:1, lanes] * sd[s, pl.ds(HIST_D + p - 2, half, stride=2), :]
                              + cdw_ref[1:2, lanes] * sd[s, pl.ds(HIST_D + p - 1, half, stride=2), :]
                              + cdw_ref[2:3, lanes] * sd[s, pl.ds(HIST_D + p, half, stride=2), :])
    yd = d_b * jnp.concatenate([od[s] for s in range(N_SLABS)], axis=-1)

    act_b = swiglu(ff_b)

    y = jnp.concatenate([v.astype(_BF16) for v in (ya, yb, yc, yd)], axis=-1)
    m = jnp.dot(y, wout_ref[...], preferred_element_type=_F32)

    x1 = x + _rms_unit(m) * (gt1 * gqm_ref[...])
    x1_s[...] = x1
    h2_s[...] = (_rms_unit(x1) * (gpf_ref[...] * (1.0 + sc2)) + sh2).astype(_BF16)

    gain2 = gt2 * gqf_ref[...]
    for r in range(0, tm, FFN_OUT_ROWS):
        rows = slice(r, r + FFN_OUT_ROWS)
        f = f_a[rows] + jnp.dot(act_b[rows], wd_ref[ff_b, :], preferred_element_type=_F32)
        xo_ref[rows, :] += _rms_unit(f) * gain2

    @pl.when(t == tiles_per_seq - 1)
    def _():
        for s in range(N_SLABS):
            lanes = slice(s * LANES, (s + 1) * LANES)
            na_ref[:, lanes] = sa[s, pl.ds(HIST_A + tm - (CONV_A - 1), CONV_A - 1), :]
            nb_ref[:, lanes] = sb[s, pl.ds(HIST_B + tm - POOL_BUF, POOL_BUF), :]
            nd_ref[:, lanes] = sd[s, pl.ds(HIST_D + tm - (CONV_D - 1), CONV_D - 1), :]

    sa[:, 0:HIST_A, :] = sa[:, tm:tm + HIST_A, :]
    sb[:, 0:HIST_B, :] = sb[:, tm:tm + HIST_B, :]
    sd[:, 0:HIST_D, :] = sd[:, tm:tm + HIST_D, :]


def _layer_spec(shape, single_buffer=False):
    zeros = (0,) * len(shape)
    kwargs = {"pipeline_mode": pl.Buffered(1)} if single_buffer else {}

    def make(layer):
        return pl.BlockSpec((None,) + tuple(shape), lambda i: (layer,) + zeros, **kwargs)
    return make


def _prompt_layer(layer, x, mod, p):
    bsz, seq, _ = x.shape
    tm = TOKEN_TILE
    nt = seq // tm
    n_tiles = bsz * nt
    n_sample = mod.shape[1] - bsz

    def vec(n):
        return _layer_spec((1, n))(layer)

    def mat(shape):
        return _layer_spec(shape, single_buffer=True)(layer)

    def mix_tile(g):
        return jnp.minimum(g, n_tiles - 1)

    def ffn_tile(g):
        return jnp.maximum(g - 1, 0)

    def state_spec(rows):
        return pl.BlockSpec((None, rows, GROUP_W), lambda g: (jnp.minimum(g // nt, bsz - 1), 0, 0))

    in_specs = [
        pl.BlockSpec((None, tm, D_MODEL), lambda g: (mix_tile(g) // nt, mix_tile(g) % nt, 0)),
        pl.BlockSpec((None, bsz, N_MOD * D_MODEL), lambda g: (layer, n_sample // bsz, 0)),
        vec(D_MODEL), vec(D_MODEL), vec(D_MODEL), vec(D_MODEL),
        mat((D_MODEL, IN_WIDTH)),
        _layer_spec((CONV_A, GROUP_W))(layer), vec(GROUP_W), vec(GROUP_W), vec(GROUP_W),
        mat((GROUP_W, GROUP_W)), vec(GROUP_W), vec(GROUP_W), vec(GROUP_W),
        _layer_spec((CHUNK, HEADS * CHUNK))(layer), _layer_spec((CHUNK, GROUP_W))(layer),
        _layer_spec((CONV_D, GROUP_W))(layer),
        mat((D_MODEL, D_MODEL)), mat((D_MODEL, D_FF)), mat((D_MODEL, D_FF)), mat((D_FF, D_MODEL)),
    ]
    out_specs = [
        pl.BlockSpec((None, tm, D_MODEL), lambda g: (ffn_tile(g) // nt, ffn_tile(g) % nt, 0)),
        state_spec(CONV_A - 1), state_spec(POOL_BUF), state_spec(CONV_D - 1),
    ]
    out_shape = [
        jax.ShapeDtypeStruct((bsz, seq, D_MODEL), _F32),
        jax.ShapeDtypeStruct((bsz, CONV_A - 1, GROUP_W), _F32),
        jax.ShapeDtypeStruct((bsz, POOL_BUF, GROUP_W), _F32),
        jax.ShapeDtypeStruct((bsz, CONV_D - 1, GROUP_W), _F32),
    ]
    return pl.pallas_call(
        functools.partial(_prompt_kernel, nt, bsz),
        grid=(n_tiles + 1,),
        in_specs=in_specs,
        out_specs=out_specs,
        out_shape=out_shape,
        scratch_shapes=[
            pltpu.VMEM((N_SLABS, HIST_A + tm, LANES), _F32),
            pltpu.VMEM((N_SLABS, HIST_B + tm, LANES), _F32),
            pltpu.VMEM((N_SLABS, HIST_D + tm, LANES), _F32),
            pltpu.VMEM((N_SLABS, tm, LANES), _F32),
            pltpu.VMEM((N_SLABS, tm, LANES), _F32),
            pltpu.VMEM((N_SLABS, tm, LANES), _F32),
            pltpu.VMEM((tm, D_MODEL), _F32),
            pltpu.VMEM((tm, D_MODEL), _BF16),
        ],
        compiler_params=pltpu.CompilerParams(
            dimension_semantics=("arbitrary",), vmem_limit_bytes=VMEM_LIMIT_BYTES),
        name=f"prompt_layer{layer}",
    )(x, mod, p["g_pre_mix"], p["g_post_mix"], p["g_pre_ffn"], p["g_post_ffn"], p["w_in"],
      p["conv_a_w"], p["conv_a_b"], p["ln_a_g"], p["ln_a_b"], p["pool_w"], p["pool_scale"],
      p["ln_c_g"], p["ln_c_b"], p["sgu_w"], p["sgu_bias"], p["conv_d_w"],
      p["w_out"], p["w_gate"], p["w_up"], p["w_down"])


def _sample_kernel(x_ref, mod_ref, sta_ref, stb_ref, std_ref, gpm_ref, gqm_ref, gpf_ref, gqf_ref,
                   win_ref, caw_ref, cab_ref, lag_ref, lab_ref, pw_ref, ps_ref, lcg_ref, lcb_ref,
                   sw0_ref, sb0_ref, cdw_ref, wout_ref, wg_ref, wu_ref, wd_ref,
                   xo_ref, na_ref, nb_ref, nd_ref, nv_ref):
    x = x_ref[...]
    mods = [mod_ref[:, i * D_MODEL:(i + 1) * D_MODEL] for i in range(N_MOD)]
    sh1, sc1 = mods[0], mods[1]
    h = _rms_unit(x) * (gpm_ref[...] * (1.0 + sc1)) + sh1
    proj = _dot(h, win_ref[...])
    a_val, a_gate, b_in, c_u, c_v, d_b, d_c, d_h = _split_proj(proj)

    za = a_val * _sigmoid(a_gate)
    acc = cab_ref[...] + caw_ref[CONV_A - 1:CONV_A, :] * za
    for k in range(CONV_A - 1):
        acc = acc + caw_ref[k:k + 1, :] * sta_ref[k]
    ya = _silu(_ln(acc, lag_ref[...], lab_ref[...]))
    for k in range(CONV_A - 2):
        na_ref[k] = sta_ref[k + 1]
    na_ref[CONV_A - 2] = za

    lane = lax.broadcasted_iota(jnp.int32, b_in.shape, 1)
    window_sum = b_in
    pooled = jnp.zeros_like(b_in)
    j = 1
    for g, w in enumerate(POOL_WINDOWS):
        while j < w:
            window_sum = window_sum + stb_ref[POOL_BUF - j]
            j += 1
        pooled = jnp.where((lane >> HEAD_SHIFT) == g, window_sum / float(w), pooled)
    yb = _dot(pooled - b_in, pw_ref[...]) * ps_ref[...]
    for k in range(POOL_BUF - 1):
        nb_ref[k] = stb_ref[k + 1]
    nb_ref[POOL_BUF - 1] = b_in

    vn = _ln(c_v, lcg_ref[...], lcb_ref[...])
    nv_ref[...] = vn
    yc = c_u * (sw0_ref[...] * vn + sb0_ref[...])

    q = d_c * d_h
    yd = d_b * (cdw_ref[0:1, :] * std_ref[0] + cdw_ref[1:2, :] * std_ref[1] + cdw_ref[2:3, :] * q)
    nd_ref[0] = std_ref[1]
    nd_ref[1] = q

    xo_ref[...] = _out_and_ffn(x, mods, (ya, yb, yc, yd), gqm_ref[...], gpf_ref[...], gqf_ref[...],
                               wout_ref, wg_ref, wu_ref, wd_ref)


def _sample_layer(layer, x, mod, sta, stb, std, p):
    n = x.shape[0]

    def vec(width):
        return _layer_spec((1, width))(layer)

    def mat(shape):
        return _layer_spec(shape, single_buffer=True)(layer)

    def state(rows):
        return pl.BlockSpec((None, rows, n, GROUP_W), lambda i: (layer, 0, 0, 0))

    in_specs = [
        pl.BlockSpec((n, D_MODEL), lambda i: (0, 0)),
        pl.BlockSpec((None, n, N_MOD * D_MODEL), lambda i: (layer, 0, 0)),
        state(CONV_A - 1), state(POOL_BUF), state(CONV_D - 1),
        vec(D_MODEL), vec(D_MODEL), vec(D_MODEL), vec(D_MODEL),
        mat((D_MODEL, IN_WIDTH)),
        _layer_spec((CONV_A, GROUP_W))(layer), vec(GROUP_W), vec(GROUP_W), vec(GROUP_W),
        mat((GROUP_W, GROUP_W)), vec(GROUP_W), vec(GROUP_W), vec(GROUP_W),
        vec(GROUP_W), vec(GROUP_W),
        _layer_spec((CONV_D, GROUP_W))(layer),
        mat((D_MODEL, D_MODEL)), mat((D_MODEL, D_FF)), mat((D_MODEL, D_FF)), mat((D_FF, D_MODEL)),
    ]
    out_specs = [
        pl.BlockSpec((n, D_MODEL), lambda i: (0, 0)),
        pl.BlockSpec((CONV_A - 1, n, GROUP_W), lambda i: (0, 0, 0)),
        pl.BlockSpec((POOL_BUF, n, GROUP_W), lambda i: (0, 0, 0)),
        pl.BlockSpec((CONV_D - 1, n, GROUP_W), lambda i: (0, 0, 0)),
        pl.BlockSpec((n, GROUP_W), lambda i: (0, 0)),
    ]
    out_shape = [
        jax.ShapeDtypeStruct((n, D_MODEL), _F32),
        jax.ShapeDtypeStruct((CONV_A - 1, n, GROUP_W), _F32),
        jax.ShapeDtypeStruct((POOL_BUF, n, GROUP_W), _F32),
        jax.ShapeDtypeStruct((CONV_D - 1, n, GROUP_W), _F32),
        jax.ShapeDtypeStruct((n, GROUP_W), _F32),
    ]
    return pl.pallas_call(
        _sample_kernel,
        grid=(1,),
        in_specs=in_specs,
        out_specs=out_specs,
        out_shape=out_shape,
        compiler_params=pltpu.CompilerParams(
            dimension_semantics=("arbitrary",), vmem_limit_bytes=VMEM_LIMIT_BYTES),
        name=f"sample_layer{layer}",
    )(x, mod, sta, stb, std, p["g_pre_mix"], p["g_post_mix"], p["g_pre_ffn"], p["g_post_ffn"],
      p["w_in"], p["conv_a_w"], p["conv_a_b"], p["ln_a_g"], p["ln_a_b"], p["pool_w"],
      p["pool_scale"], p["ln_c_g"], p["ln_c_b"], p["sgu_w00"], p["sgu_b0"], p["conv_d_w"],
      p["w_out"], p["w_gate"], p["w_up"], p["w_down"])


def _prepare_params(g_pre_mix, g_post_mix, w_in, conv_a_w, conv_a_b, ln_a_g, ln_a_b, pool_w, pool_scale,
                    ln_c_g, ln_c_b, sgu_w, sgu_b, conv_d_w, w_out, g_pre_ffn, g_post_ffn,
                    w_gate, w_up, w_down):
    n_layers = w_in.shape[0]

    def row(v):
        return v.reshape(n_layers, 1, v.shape[-1])

    n_groups = len(POOL_WINDOWS)
    eye = jnp.eye(n_groups, dtype=pool_w.dtype)
    pool_bd = jnp.einsum("lgcd,gh->lgchd", pool_w, eye).reshape(n_layers, GROUP_W, GROUP_W)
    sgu_cat = jnp.transpose(sgu_w, (0, 2, 1, 3)).reshape(n_layers, CHUNK, HEADS * CHUNK)
    sgu_bias = jnp.repeat(jnp.transpose(sgu_b, (0, 2, 1)), HEAD_DIM, axis=-1)
    return {
        "g_pre_mix": row(g_pre_mix), "g_post_mix": row(g_post_mix),
        "g_pre_ffn": row(g_pre_ffn), "g_post_ffn": row(g_post_ffn),
        "w_in": w_in.astype(_BF16), "conv_a_w": conv_a_w, "conv_a_b": row(conv_a_b),
        "ln_a_g": row(ln_a_g), "ln_a_b": row(ln_a_b),
        "pool_w": pool_bd.astype(_BF16), "pool_scale": row(pool_scale),
        "ln_c_g": row(ln_c_g), "ln_c_b": row(ln_c_b),
        "sgu_w": sgu_cat, "sgu_bias": sgu_bias,
        "sgu_w00": row(jnp.repeat(sgu_w[:, :, 0, 0], HEAD_DIM, axis=-1)),
        "sgu_b0": row(jnp.repeat(sgu_b[:, :, 0], HEAD_DIM, axis=-1)),
        "conv_d_w": conv_d_w,
        "w_out": w_out.astype(_BF16), "w_gate": w_gate.astype(_BF16),
        "w_up": w_up.astype(_BF16), "w_down": w_down.astype(_BF16),
    }


def kernel(x_prompt, x_sample, c_prompt, c_sample, state_conv_a, state_pool_b, state_conv_d, w_ada, b_ada, g_pre_mix, g_post_mix, w_in, conv_a_w, conv_a_b, ln_a_g, ln_a_b, pool_w, pool_scale, ln_c_g, ln_c_b, sgu_w, sgu_b, conv_d_w, w_out, g_pre_ffn, g_post_ffn, w_gate, w_up, w_down):
    n_layers = w_in.shape[0]
    n_sample = x_sample.shape[0]
    assert x_prompt.shape[1] % TOKEN_TILE == 0 and TOKEN_TILE % CHUNK == 0
    assert x_sample.shape[1] == 1 and n_sample % x_prompt.shape[0] == 0

    p = _prepare_params(g_pre_mix, g_post_mix, w_in, conv_a_w, conv_a_b, ln_a_g, ln_a_b, pool_w,
                        pool_scale, ln_c_g, ln_c_b, sgu_w, sgu_b, conv_d_w, w_out, g_pre_ffn,
                        g_post_ffn, w_gate, w_up, w_down)
    mod = _ada_call(jnp.concatenate([c_sample, c_prompt], axis=0), w_ada, b_ada)

    sta = jnp.transpose(state_conv_a, (0, 2, 1, 3))
    stb = jnp.transpose(state_pool_b, (0, 2, 1, 3))
    std = jnp.transpose(state_conv_d, (0, 2, 1, 3))

    xp = x_prompt
    xs = x_sample.reshape(n_sample, D_MODEL)
    outs = [[] for _ in range(7)]
    for layer in range(n_layers):
        xp, a_p, b_p, d_p = _prompt_layer(layer, xp, mod, p)
        xs, a_s, b_s, d_s, v_s = _sample_layer(layer, xs, mod, sta, stb, std, p)
        for acc, val in zip(outs, (a_p, b_p, d_p, a_s, b_s, d_s, v_s)):
            acc.append(val)

    na_p, nb_p, nd_p = (jnp.stack(o) for o in outs[:3])
    na_s, nb_s, nd_s = (jnp.transpose(jnp.stack(o), (0, 2, 1, 3)) for o in outs[3:6])
    nv_s = jnp.stack(outs[6])[:, :, None, :]
    return (xp, xs.reshape(n_sample, 1, D_MODEL), na_p, nb_p, nd_p, na_s, nb_s, nd_s, nv_s)
```
